```python
import math
import jax
import jax.numpy as jnp
from jax import lax
import numpy as np

D_MODEL = 1024
BATCH = 4
SEQ = 4096
DEPTH = 2
DEC_BATCH = 32
DEC_SEQ = 8
PAST_LEN = 8192
PAGE_SIZE = 128

N_A_LAYERS = (DEPTH + 1) // 2
N_C_LAYERS = DEPTH // 2

S5_WIDTH = D_MODEL // 2
S5_GROUP = 16
S5_GROUPS = S5_WIDTH // S5_GROUP
S5_STATE = 64
S5_DT_MIN = 0.001
S5_DT_MAX = 0.1

NSA_HEADS = 8
NSA_HEAD_DIM = 64
NSA_KV_HEADS = 2
NSA_HPG = NSA_HEADS // NSA_KV_HEADS
NSA_WIDTH = NSA_HEADS * NSA_HEAD_DIM
KV_WIDTH = NSA_KV_HEADS * NSA_HEAD_DIM
CMP_STRIDE = 16
CMP_BLOCK = 2 * CMP_STRIDE
SEL_BLOCK = 64
N_SEL = 16
WINDOW = 512
Q_BLOCK = 128
FORCE_SCORE = 1e4
ATT_SCALE = NSA_HEAD_DIM ** -0.5
IN_A = S5_WIDTH + NSA_WIDTH + 6 * KV_WIDTH + 3 * NSA_HEADS

HG_HEADS = 8
HG_KEY = D_MODEL // HG_HEADS
HG_VAL = D_MODEL // HG_HEADS
HG_CHUNK = 64
IN_C = 2 * HG_HEADS * HG_KEY + 2 * HG_HEADS * HG_VAL

PEER_HEADS = 8
PEER_NKEYS = 128
PEER_EXPERTS = PEER_NKEYS * PEER_NKEYS
PEER_QDIM = 256
PEER_HALF = PEER_QDIM // 2
PEER_TOPK = 16
PEER_TOKEN_BLOCK = 128

EPS = 1e-6
NEG = -1e30

kernel_name = 'nsa_s5_hgrn2_peer_hybrid_step'


def _rmsnorm(x, g):
    x32 = x.astype(jnp.float32)
    y = x32 * lax.rsqrt(jnp.mean(x32 * x32, axis=-1, keepdims=True) + EPS)
    return (y * g.astype(jnp.float32)).astype(x.dtype)


def _ada(c, w, b):
    m = jax.nn.silu(c.astype(jnp.float32)) @ w.astype(jnp.float32) + b.astype(jnp.float32)
    return [part[:, None, :].astype(c.dtype) for part in jnp.split(m, 6, axis=-1)]


def _linear_combine(left, right):
    a1, b1 = left
    a2, b2 = right
    return a1 * a2, a2 * b1 + b2


def _s5(u, h0, lam_re, lam_im, log_dt, b_re, b_im, c_re, c_im, d, glu_w, glu_b):
    f32 = jnp.float32
    bsz, t, _ = u.shape
    ug = u.astype(f32).reshape(bsz, t, S5_GROUPS, S5_GROUP)
    lam = lax.complex(lam_re.astype(f32), lam_im.astype(f32))
    dt = jnp.exp(log_dt.astype(f32))[:, None]
    lam_bar = jnp.exp(lam * dt)
    b_bar = ((lam_bar - 1.0) / lam)[..., None] * lax.complex(b_re.astype(f32), b_im.astype(f32))
    bu = jnp.einsum('btgi,gpi->btgp', ug.astype(jnp.complex64), b_bar)
    a = jnp.broadcast_to(lam_bar, bu.shape)
    a_cum, h = lax.associative_scan(_linear_combine, (a, bu), axis=1)
    h = h + a_cum * h0[:, None]
    c = lax.complex(c_re.astype(f32), c_im.astype(f32))
    y = jnp.einsum('btgp,gip->btgi', h, c).real + d.astype(f32).reshape(S5_GROUPS, S5_GROUP) * ug
    y = jax.nn.gelu(y.reshape(bsz, t, S5_WIDTH))
    out = y * jax.nn.sigmoid(y @ glu_w.astype(f32) + glu_b.astype(f32))
    return out, h[:, -1]


def _masked_attn(q, k, v, mask):
    s = jnp.einsum('bmqghd,bmkgd->bmghqk', q.astype(jnp.float32), k.astype(jnp.float32)) * ATT_SCALE
    p = jax.nn.softmax(jnp.where(mask[None, :, None, None], s, NEG), axis=-1)
    return jnp.einsum('bmghqk,bmkgd->bmqghd', p, v.astype(jnp.float32))


def _window_branch(q, k, v, k_pos0):
    bsz, t = q.shape[:2]
    tw = k.shape[1]
    q_pos0 = k_pos0 + tw - t
    if tw == t and t % Q_BLOCK == 0 and WINDOW % Q_BLOCK == 0:
        nb = t // Q_BLOCK
        r = WINDOW // Q_BLOCK
        pad = ((0, 0), (r * Q_BLOCK, 0), (0, 0), (0, 0))
        kb = jnp.pad(k, pad).reshape(bsz, nb + r, Q_BLOCK, NSA_KV_HEADS, NSA_HEAD_DIM)
        vb = jnp.pad(v, pad).reshape(bsz, nb + r, Q_BLOCK, NSA_KV_HEADS, NSA_HEAD_DIM)
        kband = jnp.concatenate([kb[:, i:i + nb] for i in range(r + 1)], axis=2)
        vband = jnp.concatenate([vb[:, i:i + nb] for i in range(r + 1)], axis=2)
        qpos = q_pos0 + jnp.arange(t).reshape(nb, Q_BLOCK)
        kpos = q_pos0 + (jnp.arange(nb)[:, None] - r) * Q_BLOCK + jnp.arange((r + 1) * Q_BLOCK)[None, :]
        kp = kpos[:, None, :]
        qp = qpos[:, :, None]
        mask = (kp <= qp) & (kp > qp - WINDOW) & (kp >= q_pos0)
        qb = q.reshape(bsz, nb, Q_BLOCK, NSA_KV_HEADS, NSA_HPG, NSA_HEAD_DIM)
        return _masked_attn(qb, kband, vband, mask).reshape(bsz, t, NSA_KV_HEADS, NSA_HPG, NSA_HEAD_DIM)
    qpos = q_pos0 + jnp.arange(t)
    kpos = k_pos0 + jnp.arange(tw)
    mask = (kpos[None, :] <= qpos[:, None]) & (kpos[None, :] > qpos[:, None] - WINDOW)
    return _masked_attn(q[:, None], k[:, None], v[:, None], mask[None])[:, 0]


def _cmp_sel_branch(q, kc_raw, vc_raw, ks_raw, vs_raw, w_ck, w_cv):
    f32 = jnp.float32
    bsz, t = q.shape[:2]
    tk = kc_raw.shape[1]
    q_pos = tk - t + jnp.arange(t)
    q32 = q.astype(f32)
    nch = -(-tk // CMP_STRIDE)

    def blocks(x):
        x = jnp.pad(x, ((0, 0), (0, nch * CMP_STRIDE - tk), (0, 0), (0, 0)))
        x = x.reshape(bsz, nch, CMP_STRIDE, NSA_KV_HEADS, NSA_HEAD_DIM)
        return jnp.concatenate([x[:, :-1], x[:, 1:]], axis=2)

    kc = jnp.einsum('bnjgd,jde->bnge', blocks(kc_raw).astype(f32), w_ck.astype(f32))
    vc = jnp.einsum('bnjgd,jde->bnge', blocks(vc_raw).astype(f32), w_cv.astype(f32))
    n_cmp = nch - 1
    cmp_start = jnp.arange(n_cmp) * CMP_STRIDE
    mask_c = (cmp_start + CMP_BLOCK - 1)[None, :] <= q_pos[:, None]
    s = jnp.einsum('bqghd,bngd->bghqn', q32, kc) * ATT_SCALE
    p = jax.nn.softmax(jnp.where(mask_c, s, NEG), axis=-1) * mask_c.astype(f32)
    o_cmp = jnp.einsum('bghqn,bngd->bqghd', p, vc)
    n_sel = -(-tk // SEL_BLOCK)
    blk = jnp.arange(n_sel)
    sel_start = blk * SEL_BLOCK
    overlap = jnp.clip(jnp.minimum(cmp_start[:, None] + CMP_BLOCK, sel_start[None, :] + SEL_BLOCK)
                       - jnp.maximum(cmp_start[:, None], sel_start[None, :]), 0, None).astype(f32) / CMP_BLOCK
    imp = jnp.einsum('bghqn,nj->bgqj', p, overlap)
    cur = (q_pos // SEL_BLOCK)[:, None]
    valid = sel_start[None, :] <= q_pos[:, None]
    forced = (blk[None, :] == 0) | (blk[None, :] == cur) | (blk[None, :] == cur - 1)
    score = jnp.where(forced, FORCE_SCORE, jnp.where(valid, imp, -1.0))
    n_top = min(N_SEL, n_sel)
    top_s, top_i = lax.top_k(score, n_top)
    top_ok = top_s >= 0.0
    def sel_blocks(x):
        x = jnp.pad(x, ((0, 0), (0, n_sel * SEL_BLOCK - tk), (0, 0), (0, 0)))
        return x.reshape(bsz, n_sel, SEL_BLOCK, NSA_KV_HEADS, NSA_HEAD_DIM).transpose(0, 3, 1, 2, 4)

    ksb = sel_blocks(ks_raw)
    vsb = sel_blocks(vs_raw)
    qb = Q_BLOCK if t % Q_BLOCK == 0 else t
    nq = t // qb
    xs = (q32.reshape(bsz, nq, qb, NSA_KV_HEADS, NSA_HPG, NSA_HEAD_DIM).swapaxes(0, 1),
          top_i.reshape(bsz, NSA_KV_HEADS, nq, qb, n_top).transpose(2, 0, 1, 3, 4),
          top_ok.reshape(bsz, NSA_KV_HEADS, nq, qb, n_top).transpose(2, 0, 1, 3, 4),
          q_pos.reshape(nq, qb))
    gather = jax.vmap(jax.vmap(lambda blocks_bg, idx_bg: blocks_bg[idx_bg]))

    def sel_block(args):
        qq, ii, ok, pos = args
        kg = gather(ksb, ii).astype(f32)
        vg = gather(vsb, ii).astype(f32)
        kpos = ii[..., None] * SEL_BLOCK + jnp.arange(SEL_BLOCK)
        m = ok[..., None] & (kpos <= pos[None, None, :, None, None])
        sc = jnp.einsum('bqghd,bgqnjd->bghqnj', qq, kg) * ATT_SCALE
        sc = jnp.where(m[:, :, None], sc, NEG).reshape(bsz, NSA_KV_HEADS, NSA_HPG, qb, n_top * SEL_BLOCK)
        pp = jax.nn.softmax(sc, axis=-1).reshape(bsz, NSA_KV_HEADS, NSA_HPG, qb, n_top, SEL_BLOCK)
        return jnp.einsum('bghqnj,bgqnjd->bqghd', pp, vg)

    o_sel = lax.map(sel_block, xs).swapaxes(0, 1).reshape(bsz, t, NSA_KV_HEADS, NSA_HPG, NSA_HEAD_DIM)
    return o_cmp, o_sel


def _mixer_a(h, past_cmp, past_sel, past_win, s5_h0, w_in, w_out, w_ck, w_cv,
             lam_re, lam_im, log_dt, b_re, b_im, c_re, c_im, d, glu_w, glu_b):
    bsz, t, _ = h.shape
    z = h @ w_in
    sizes = [S5_WIDTH, NSA_WIDTH] + [KV_WIDTH] * 6 + [3 * NSA_HEADS]
    u, qf, kc, vc, ks, vs, kw, vw, gz = jnp.split(z, np.cumsum(sizes)[:-1].tolist(), axis=-1)
    kvr = lambda a: a.reshape(bsz, t, NSA_KV_HEADS, NSA_HEAD_DIM)
    q = qf.reshape(bsz, t, NSA_KV_HEADS, NSA_HPG, NSA_HEAD_DIM)
    gates = jax.nn.sigmoid(gz.astype(jnp.float32)).reshape(bsz, t, NSA_KV_HEADS, NSA_HPG, 3)
    new_cmp = jnp.stack([kvr(kc), kvr(vc)], axis=2)
    new_sel = jnp.stack([kvr(ks), kvr(vs)], axis=2)
    new_w = jnp.stack([kvr(kw), kvr(vw)], axis=2)
    all_cmp = jnp.concatenate([past_cmp.astype(h.dtype), new_cmp], axis=1)
    all_sel = jnp.concatenate([past_sel.astype(h.dtype), new_sel], axis=1)
    all_win = jnp.concatenate([past_win.astype(h.dtype), new_w], axis=1)
    tk = all_cmp.shape[1]
    o_cmp, o_sel = _cmp_sel_branch(q, all_cmp[:, :, 0], all_cmp[:, :, 1], all_sel[:, :, 0], all_sel[:, :, 1], w_ck, w_cv)
    o_win = _window_branch(q, all_win[:, :, 0], all_win[:, :, 1], tk - all_win.shape[1])
    o = gates[..., 0:1] * o_cmp + gates[..., 1:2] * o_sel + gates[..., 2:3] * o_win
    y_s5, h_last = _s5(u, s5_h0, lam_re, lam_im, log_dt, b_re, b_im, c_re, c_im, d, glu_w, glu_b)
    mixed = jnp.concatenate([y_s5, o.reshape(bsz, t, NSA_WIDTH)], axis=-1).astype(h.dtype)
    out = mixed @ w_out
    new_win = all_win[:, -min(WINDOW, all_win.shape[1]):]
    s5_state = jnp.stack([h_last.real, h_last.imag], axis=-1)
    return out, new_cmp, new_sel, new_win, s5_state


def _hgrn_chunked(q, k, v, logf, s0):
    bsz, t, nh, dk = q.shape
    dv = v.shape[-1]
    L = min(HG_CHUNK, t)
    nc = -(-t // L)
    pad = nc * L - t

    def prep(x):
        x = jnp.pad(x, ((0, 0), (0, pad), (0, 0), (0, 0)))
        return x.reshape(bsz, nc, L, nh, x.shape[-1]).transpose(1, 0, 3, 2, 4)

    causal = jnp.tril(jnp.ones((L, L), dtype=bool))

    def step(S, xs):
        qc, kc, vc, lc = xs
        b = jnp.cumsum(lc, axis=2)
        o_inter = jnp.einsum('bhld,bhde->bhle', qc * jnp.exp(b), S)
        diff = b[:, :, :, None, :] - b[:, :, None, :, :]
        decay = jnp.exp(jnp.where(causal[:, :, None], diff, -jnp.inf))
        att = jnp.einsum('bhtd,bhtsd,bhsd->bhts', qc, decay, kc)
        o = o_inter + jnp.einsum('bhts,bhse->bhte', att, vc)
        b_last = b[:, :, -1:, :]
        S_new = jnp.exp(b[:, :, -1, :])[..., None] * S + jnp.einsum('bhsd,bhse->bhde', kc * jnp.exp(b_last - b), vc)
        return S_new, o

    s_last, o = lax.scan(step, s0, (prep(q), prep(k), prep(v), prep(logf)))
    o = o.transpose(1, 0, 3, 2, 4).reshape(bsz, nc * L, nh, dv)[:, :t]
    return o, s_last


def _mixer_c(h, s0, w_in, w_out, lb, norm_g):
    f32 = jnp.float32
    bsz, t, _ = h.shape
    qf, fz, i_in, g = jnp.split(h @ w_in, 4, axis=-1)
    q = qf.astype(f32).reshape(bsz, t, HG_HEADS, HG_KEY)
    fz = fz.astype(f32).reshape(bsz, t, HG_HEADS, HG_KEY)
    v = i_in.astype(f32).reshape(bsz, t, HG_HEADS, HG_VAL)
    lbh = lb.astype(f32).reshape(HG_HEADS, HG_KEY)
    f = lbh + (1.0 - lbh) * jax.nn.sigmoid(fz)
    k = (1.0 - lbh) * jax.nn.sigmoid(-fz)
    o, s_last = _hgrn_chunked(q, k, v, jnp.log(f), s0)
    o = o * lax.rsqrt(jnp.mean(o * o, axis=-1, keepdims=True) + EPS) * norm_g.astype(f32)
    o = o * jax.nn.silu(g.astype(f32).reshape(bsz, t, HG_HEADS, HG_VAL))
    return o.reshape(bsz, t, HG_HEADS * HG_VAL).astype(h.dtype) @ w_out, s_last


def _peer(h, wq, k1, k2, u_tab, v_tab):
    f32 = jnp.float32
    bsz, t, d = h.shape
    n = bsz * t
    x = h.reshape(n, d)
    q = (x @ wq).astype(f32).reshape(n, PEER_HEADS, PEER_QDIM)
    s1 = jnp.einsum('nhc,kc->nhk', q[..., :PEER_HALF], k1.astype(f32))
    s2 = jnp.einsum('nhc,kc->nhk', q[..., PEER_HALF:], k2.astype(f32))
    v1, i1 = lax.top_k(s1, PEER_TOPK)
    v2, i2 = lax.top_k(s2, PEER_TOPK)
    cand = (v1[..., :, None] + v2[..., None, :]).reshape(n, PEER_HEADS, PEER_TOPK * PEER_TOPK)
    cid = (i1[..., :, None] * PEER_NKEYS + i2[..., None, :]).reshape(n, PEER_HEADS, PEER_TOPK * PEER_TOPK)
    top_v, top_j = lax.top_k(cand, PEER_TOPK)
    eid = jnp.take_along_axis(cid, top_j, axis=-1)
    gate = jax.nn.softmax(top_v, axis=-1)
    tb = PEER_TOKEN_BLOCK
    nb = -(-n // tb)
    pad = nb * tb - n
    xs = (jnp.pad(x, ((0, pad), (0, 0))).reshape(nb, tb, d),
          jnp.pad(eid, ((0, pad), (0, 0), (0, 0))).reshape(nb, tb, PEER_HEADS, PEER_TOPK),
          jnp.pad(gate, ((0, pad), (0, 0), (0, 0))).reshape(nb, tb, PEER_HEADS, PEER_TOPK))

    def block(args):
        xb, eb, gb = args
        act = jax.nn.gelu(jnp.einsum('td,thkd->thk', xb.astype(f32), u_tab[eb].astype(f32)))
        return jnp.einsum('thk,thkd->td', gb * act, v_tab[eb].astype(f32))

    y = lax.map(block, xs).reshape(nb * tb, d)[:n]
    return y.reshape(bsz, t, d).astype(h.dtype)


def setup_inputs(seed: int = 0) -> dict:
    f32 = jnp.float32
    key = jax.random.key(seed)
    ks = list(jax.random.split(key, 48))

    def nrm(shape, scale):
        return scale * jax.random.normal(ks.pop(), shape, f32)

    n_pages = PAST_LEN // PAGE_SIZE
    n_used = DEC_BATCH * n_pages
    n_phys = n_used + n_used // 4
    wb = min(WINDOW, PAST_LEN)
    perm = jax.random.permutation(ks.pop(), n_phys)
    page_table = perm[:n_used].reshape(DEC_BATCH, n_pages).astype(jnp.int32)
    kvd = (2, NSA_KV_HEADS, NSA_HEAD_DIM)
    lam_im = jnp.pi * jnp.arange(S5_STATE, dtype=f32) + nrm((N_A_LAYERS, S5_GROUPS, S5_STATE), 0.01)
    log_dt = jax.random.uniform(ks.pop(), (N_A_LAYERS, S5_GROUPS), f32, math.log(S5_DT_MIN), math.log(S5_DT_MAX))
    return {
        'x_prompt': nrm((BATCH, SEQ, D_MODEL), 1.0),
        'x_sample': nrm((DEC_BATCH, DEC_SEQ, D_MODEL), 1.0),
        'cache_cmp_kv': nrm((N_A_LAYERS, n_phys, PAGE_SIZE) + kvd, 1.0),
        'cache_sel_kv': nrm((N_A_LAYERS, n_phys, PAGE_SIZE) + kvd, 1.0),
        'cache_win_kv': nrm((N_A_LAYERS, DEC_BATCH, wb) + kvd, 1.0),
        'state_s5': nrm((N_A_LAYERS, DEC_BATCH, S5_GROUPS, S5_STATE, 2), 0.1),
        'state_hgrn': nrm((N_C_LAYERS, DEC_BATCH, HG_HEADS, HG_KEY, HG_VAL), 0.5),
        'page_table': page_table,
        'c_prompt': nrm((BATCH, D_MODEL), 1.0),
        'c_sample': nrm((DEC_BATCH, D_MODEL), 1.0),
        'ada_w': nrm((DEPTH, D_MODEL, 6 * D_MODEL), 0.5 * D_MODEL ** -0.5),
        'ada_b': nrm((DEPTH, 6 * D_MODEL), 0.02),
        'norm_mix_g': 1.0 + nrm((DEPTH, D_MODEL), 0.02),
        'norm_ffn_g': 1.0 + nrm((DEPTH, D_MODEL), 0.02),
        'norm_final_g': 1.0 + nrm((D_MODEL,), 0.02),
        'a_w_in': nrm((N_A_LAYERS, D_MODEL, IN_A), D_MODEL ** -0.5),
        'a_w_out': nrm((N_A_LAYERS, S5_WIDTH + NSA_WIDTH, D_MODEL), (S5_WIDTH + NSA_WIDTH) ** -0.5),
        's5_lambda_re': -0.5 + nrm((N_A_LAYERS, S5_GROUPS, S5_STATE), 0.01),
        's5_lambda_im': lam_im,
        's5_log_dt': log_dt,
        's5_b_re': nrm((N_A_LAYERS, S5_GROUPS, S5_STATE, S5_GROUP), (2 * S5_GROUP) ** -0.5),
        's5_b_im': nrm((N_A_LAYERS, S5_GROUPS, S5_STATE, S5_GROUP), (2 * S5_GROUP) ** -0.5),
        's5_c_re': nrm((N_A_LAYERS, S5_GROUPS, S5_GROUP, S5_STATE), (2 * S5_STATE) ** -0.5),
        's5_c_im': nrm((N_A_LAYERS, S5_GROUPS, S5_GROUP, S5_STATE), (2 * S5_STATE) ** -0.5),
        's5_d': nrm((N_A_LAYERS, S5_WIDTH), 1.0),
        's5_glu_w': nrm((N_A_LAYERS, S5_WIDTH, S5_WIDTH), S5_WIDTH ** -0.5),
        's5_glu_b': nrm((N_A_LAYERS, S5_WIDTH), 0.02),
        'nsa_cmp_wk': nrm((N_A_LAYERS, CMP_BLOCK, NSA_HEAD_DIM, NSA_HEAD_DIM), (CMP_BLOCK * NSA_HEAD_DIM) ** -0.5),
        'nsa_cmp_wv': nrm((N_A_LAYERS, CMP_BLOCK, NSA_HEAD_DIM, NSA_HEAD_DIM), (CMP_BLOCK * NSA_HEAD_DIM) ** -0.5),
        'c_w_in': nrm((N_C_LAYERS, D_MODEL, IN_C), D_MODEL ** -0.5),
        'c_w_out': nrm((N_C_LAYERS, HG_HEADS * HG_VAL, D_MODEL), (HG_HEADS * HG_VAL) ** -0.5),
        'hgrn_lb': nrm((DEPTH, HG_HEADS * HG_KEY), 0.5),
        'hgrn_norm_g': 1.0 + nrm((N_C_LAYERS, HG_VAL), 0.02),
        'peer_wq': nrm((DEPTH, D_MODEL, PEER_HEADS * PEER_QDIM), D_MODEL ** -0.5),
        'peer_k1': nrm((DEPTH, PEER_NKEYS, PEER_HALF), PEER_HALF ** -0.5),
        'peer_k2': nrm((DEPTH, PEER_NKEYS, PEER_HALF), PEER_HALF ** -0.5),
        'peer_u': nrm((DEPTH, PEER_EXPERTS, D_MODEL), D_MODEL ** -0.5),
        'peer_v': nrm((DEPTH, PEER_EXPERTS, D_MODEL), PEER_HEADS ** -0.5),
    }


def reference(x_prompt, x_sample, cache_cmp_kv, cache_sel_kv, cache_win_kv, state_s5, state_hgrn, page_table,
              c_prompt, c_sample, ada_w, ada_b, norm_mix_g, norm_ffn_g, norm_final_g, a_w_in, a_w_out,
              s5_lambda_re, s5_lambda_im, s5_log_dt, s5_b_re, s5_b_im, s5_c_re, s5_c_im, s5_d, s5_glu_w, s5_glu_b,
              nsa_cmp_wk, nsa_cmp_wv, c_w_in, c_w_out, hgrn_lb, hgrn_norm_g,
              peer_wq, peer_k1, peer_k2, peer_u, peer_v):
    f32 = jnp.float32
    dec_b, n_pages = page_table.shape
    bsz = x_prompt.shape[0]

    def gather_past(pool):
        rows = pool[page_table]
        return rows.reshape(dec_b, n_pages * PAGE_SIZE, 2, NSA_KV_HEADS, NSA_HEAD_DIM)

    lb_sm = jax.nn.softmax(hgrn_lb.astype(f32), axis=0)
    lb_all = jnp.cumsum(lb_sm, axis=0) - lb_sm[0]

    xp, xs = x_prompt, x_sample
    cmp_p, cmp_s, sel_p, sel_s, win_p, win_s, s5_p, s5_s, hg_p, hg_s = [], [], [], [], [], [], [], [], [], []
    for l in range(DEPTH):
        shp_m, scp_m, gtp_m, shp_f, scp_f, gtp_f = _ada(c_prompt, ada_w[l], ada_b[l])
        shs_m, scs_m, gts_m, shs_f, scs_f, gts_f = _ada(c_sample, ada_w[l], ada_b[l])
        hp = _rmsnorm(xp, norm_mix_g[l]) * (1.0 + scp_m) + shp_m
        hs = _rmsnorm(xs, norm_mix_g[l]) * (1.0 + scs_m) + shs_m
        if l % 2 == 0:
            a = l // 2
            wa = (a_w_in[a], a_w_out[a], nsa_cmp_wk[a], nsa_cmp_wv[a], s5_lambda_re[a], s5_lambda_im[a], s5_log_dt[a],
                  s5_b_re[a], s5_b_im[a], s5_c_re[a], s5_c_im[a], s5_d[a], s5_glu_w[a], s5_glu_b[a])
            empty = jnp.zeros((bsz, 0, 2, NSA_KV_HEADS, NSA_HEAD_DIM), xp.dtype)
            h0p = jnp.zeros((bsz, S5_GROUPS, S5_STATE), jnp.complex64)
            op, n_cp, n_sp, n_wp, n_5p = _mixer_a(hp, empty, empty, empty, h0p, *wa)
            st = state_s5[a]
            h0s = lax.complex(st[..., 0].astype(f32), st[..., 1].astype(f32))
            os_, n_cs, n_ss, n_ws, n_5s = _mixer_a(hs, gather_past(cache_cmp_kv[a]), gather_past(cache_sel_kv[a]),
                                                   cache_win_kv[a], h0s, *wa)
            cmp_p.append(n_cp)
            cmp_s.append(n_cs)
            sel_p.append(n_sp)
            sel_s.append(n_ss)
            win_p.append(n_wp)
            win_s.append(n_ws)
            s5_p.append(n_5p)
            s5_s.append(n_5s)
        else:
            j = l // 2
            wc = (c_w_in[j], c_w_out[j], lb_all[l], hgrn_norm_g[j])
            op, n_hp = _mixer_c(hp, jnp.zeros((bsz, HG_HEADS, HG_KEY, HG_VAL), f32), *wc)
            os_, n_hs = _mixer_c(hs, state_hgrn[j].astype(f32), *wc)
            hg_p.append(n_hp)
            hg_s.append(n_hs)
        xp = xp + gtp_m * op.astype(xp.dtype)
        xs = xs + gts_m * os_.astype(xs.dtype)
        wp = (peer_wq[l], peer_k1[l], peer_k2[l], peer_u[l], peer_v[l])
        xp = xp + gtp_f * _peer(_rmsnorm(xp, norm_ffn_g[l]) * (1.0 + scp_f) + shp_f, *wp)
        xs = xs + gts_f * _peer(_rmsnorm(xs, norm_ffn_g[l]) * (1.0 + scs_f) + shs_f, *wp)
    y_prompt = _rmsnorm(xp, norm_final_g)
    y_sample = _rmsnorm(xs, norm_final_g)
    return (y_prompt, y_sample, jnp.stack(cmp_p), jnp.stack(cmp_s), jnp.stack(sel_p), jnp.stack(sel_s),
            jnp.stack(win_p), jnp.stack(win_s), jnp.stack(s5_p), jnp.stack(s5_s), jnp.stack(hg_p), jnp.stack(hg_s))
```

```python
import functools
import math

import jax
import jax.numpy as jnp
import numpy as np
from jax import lax
from jax.experimental import pallas as pl
from jax.experimental.pallas import tpu as pltpu

f32 = jnp.float32
bf16 = jnp.bfloat16

D_MODEL = 1024
EPS = 1e-6
NEG = -1e30

PEER_HEADS = 8
PEER_NKEYS = 128
PEER_QDIM = 256
PEER_HALF = PEER_QDIM // 2
PEER_TOPK = 16
PEER_EXPERT_TILE = 1024
PEER_ROWS_PER_TILE = PEER_EXPERT_TILE // PEER_NKEYS

VMEM_LIMIT = 56 * 1024 * 1024


def _cparams(*sem):
    return pltpu.CompilerParams(dimension_semantics=sem, vmem_limit_bytes=VMEM_LIMIT)


def _mm(a, b):
    return jnp.dot(a.astype(bf16), b.astype(bf16), preferred_element_type=f32)


def _mm_nt(a, b):
    return lax.dot_general(a.astype(bf16), b.astype(bf16), (((1,), (1,)), ((), ())), preferred_element_type=f32)


def _split(w):
    hi = w.astype(bf16)
    return hi, (w - hi.astype(f32)).astype(bf16)


def _mm3(a, b_hi, b_lo):
    a_hi, a_lo = _split(a)
    return _mm(a_hi, b_hi) + (_mm(a_lo, b_hi) + _mm(a_hi, b_lo))


def _gelu_tanh(x):
    return 0.5 * x * (1.0 + jnp.tanh(0.7978845608028654 * (x + 0.044715 * (x * x * x))))


def _norm_mod(x, g, scale, shift):
    y = x * lax.rsqrt(jnp.mean(x * x, axis=-1, keepdims=True) + EPS) * g
    return y * (1.0 + scale) + shift


def _mod_spec(mod, tn, tiles_per_group):
    return pl.BlockSpec((None, mod.shape[1], mod.shape[2]), lambda t: (t // tiles_per_group, 0, 0))


def _top_values(s, k):
    vals = []
    for _ in range(k):
        m = jnp.max(s, axis=0, keepdims=True)
        vals.append(m)
        s = jnp.where(s >= m, -jnp.inf, s)
    return vals


_PEER_NCAND = PEER_TOPK + 1
_PEER_PAIRS = [(a, b) for a in range(_PEER_NCAND) for b in range(_PEER_NCAND) if (a + 1) * (b + 1) <= _PEER_NCAND]
_PEER_CAND_ROWS = -(-len(_PEER_PAIRS) // 8) * 8


def _peer_prep_kernel(x_ref, scale_ref, shift_ref, g_ref, wqh_ref, wql_ref, kh_ref, kl_ref,
                      ht_ref, s2_ref, e2_ref, th_ref, c_ref, cand_ref):
    h = _norm_mod(x_ref[...], g_ref[...], scale_ref[...], shift_ref[...])
    ht_ref[...] = h.T.astype(bf16)
    q = _mm3(h, wqh_ref[...], wql_ref[...])
    cand_ref[...] = jnp.full(cand_ref.shape, -jnp.inf, f32)
    for hd in range(PEER_HEADS):
        qh = q[:, hd * PEER_QDIM:(hd + 1) * PEER_QDIM]
        st = _mm3(qh, kh_ref[...], kl_ref[...]).T
        s1 = st[:PEER_NKEYS]
        s2 = st[PEER_NKEYS:]
        v1 = _top_values(s1, _PEER_NCAND)
        v2 = _top_values(s2, _PEER_NCAND)
        for r, (a, b) in enumerate(_PEER_PAIRS):
            cand_ref[r:r + 1, :] = v1[a] + v2[b]
        cand = cand_ref[...]
        cv = _top_values(cand, _PEER_NCAND)
        tau = 0.5 * (cv[PEER_TOPK - 1] + cv[PEER_TOPK])
        top = v1[0] + v2[0]
        z = jnp.sum(jnp.where(cand >= tau, jnp.exp(cand - top), 0.0), axis=0, keepdims=True)
        s2_ref[hd] = s2
        e2_ref[hd] = jnp.exp(s2 - v2[0])
        th_ref[hd] = tau - s1
        c_ref[hd] = jnp.exp(s1 - v1[0]) / z


def _peer_prep(x, scale, shift, g, wq_hl, kbd_hl, tn, tiles_per_group):
    n, d = x.shape
    nt = n // tn
    hk = (PEER_HEADS, PEER_NKEYS, n)
    tile = pl.BlockSpec((PEER_HEADS, PEER_NKEYS, tn), lambda t: (0, 0, t))
    return pl.pallas_call(
        _peer_prep_kernel,
        grid=(nt,),
        in_specs=[pl.BlockSpec((tn, d), lambda t: (t, 0)),
                  _mod_spec(scale, tn, tiles_per_group), _mod_spec(shift, tn, tiles_per_group),
                  pl.BlockSpec((1, d), lambda t: (0, 0)),
                  pl.BlockSpec(wq_hl[0].shape, lambda t: (0, 0)), pl.BlockSpec(wq_hl[1].shape, lambda t: (0, 0)),
                  pl.BlockSpec(kbd_hl[0].shape, lambda t: (0, 0)), pl.BlockSpec(kbd_hl[1].shape, lambda t: (0, 0))],
        out_specs=[pl.BlockSpec((d, tn), lambda t: (0, t)), tile, tile, tile, tile],
        out_shape=[jax.ShapeDtypeStruct((d, n), bf16)] + [jax.ShapeDtypeStruct(hk, f32)] * 4,
        scratch_shapes=[pltpu.VMEM((_PEER_CAND_ROWS, tn), f32)],
        compiler_params=_cparams("parallel"),
        name="peer_prep",
    )(x, scale, shift, g, *wq_hl, *kbd_hl)


def _peer_expert_kernel(u_ref, vt_ref, ht_ref, s2_ref, e2_ref, th_ref, c_ref, yt_ref, aw_ref):
    e = pl.program_id(1)

    @pl.when(e == 0)
    def _():
        yt_ref[...] = jnp.zeros_like(yt_ref)

    act = _gelu_tanh(_mm(u_ref[...], ht_ref[...]))
    for il in range(PEER_ROWS_PER_TILE):
        w = None
        for hd in range(PEER_HEADS):
            keep = s2_ref[hd] >= th_ref[hd, il:il + 1, :]
            contrib = jnp.where(keep, e2_ref[hd] * c_ref[hd, il:il + 1, :], 0.0)
            w = contrib if w is None else w + contrib
        rows = slice(il * PEER_NKEYS, (il + 1) * PEER_NKEYS)
        aw_ref[rows, :] = (act[rows, :] * w).astype(bf16)
    yt_ref[...] += _mm(vt_ref[...], aw_ref[...])


def _peer_experts(u_bf, vt_bf, ht, s2t, e2t, tht, ct, tn):
    d, n = ht.shape
    ne = u_bf.shape[0] // PEER_EXPERT_TILE
    et, rt = PEER_EXPERT_TILE, PEER_ROWS_PER_TILE
    keys = pl.BlockSpec((PEER_HEADS, PEER_NKEYS, tn), lambda t, e: (0, 0, t))
    rows = pl.BlockSpec((PEER_HEADS, rt, tn), lambda t, e: (0, e, t))
    return pl.pallas_call(
        _peer_expert_kernel,
        grid=(n // tn, ne),
        in_specs=[pl.BlockSpec((et, d), lambda t, e: (e, 0)),
                  pl.BlockSpec((d, et), lambda t, e: (0, e)),
                  pl.BlockSpec((d, tn), lambda t, e: (0, t)),
                  keys, keys, rows, rows],
        out_specs=pl.BlockSpec((d, tn), lambda t, e: (0, t)),
        out_shape=jax.ShapeDtypeStruct((d, n), f32),
        scratch_shapes=[pltpu.VMEM((et, tn), bf16)],
        compiler_params=_cparams("parallel", "arbitrary"),
        name="peer_experts",
    )(u_bf, vt_bf, ht, s2t, e2t, tht, ct)


def _peer_finish_kernel(x_ref, gate_ref, yt_ref, o_ref):
    o_ref[...] = x_ref[...] + gate_ref[...] * yt_ref[...].T


def _peer_finish_norm_kernel(x_ref, gate_ref, yt_ref, g_ref, o_ref):
    x = x_ref[...] + gate_ref[...] * yt_ref[...].T
    o_ref[...] = x * lax.rsqrt(jnp.mean(x * x, axis=-1, keepdims=True) + EPS) * g_ref[...]


def _peer_finish(x, gate, yt, tn, tiles_per_group, final_g=None):
    n, d = x.shape
    in_specs = [pl.BlockSpec((tn, d), lambda t: (t, 0)), _mod_spec(gate, tn, tiles_per_group),
                pl.BlockSpec((d, tn), lambda t: (0, t))]
    args = [x, gate, yt]
    body = _peer_finish_kernel
    if final_g is not None:
        in_specs.append(pl.BlockSpec((1, d), lambda t: (0, 0)))
        args.append(final_g)
        body = _peer_finish_norm_kernel
    return pl.pallas_call(
        body,
        grid=(n // tn,),
        in_specs=in_specs,
        out_specs=pl.BlockSpec((tn, d), lambda t: (t, 0)),
        out_shape=jax.ShapeDtypeStruct((n, d), f32),
        compiler_params=_cparams("parallel"),
        name="peer_finish",
    )(*args)


def _peer_weights(wq, k1, k2, u_tab, v_tab):
    zero = jnp.zeros((PEER_HALF, PEER_NKEYS), f32)
    kbd = jnp.block([[k1.T, zero], [zero, k2.T]])
    return _split(wq), _split(kbd), u_tab.astype(bf16), v_tab.T.astype(bf16)


def _peer_layer(x, scale, shift, gate, norm_g, weights, tn, tiles_per_group, final_g=None):
    wq_hl, kbd_hl, u_bf, vt_bf = weights
    ht, s2t, e2t, tht, ct = _peer_prep(x, scale, shift, norm_g, wq_hl, kbd_hl, tn, tiles_per_group)
    yt = _peer_experts(u_bf, vt_bf, ht, s2t, e2t, tht, ct, tn)
    return _peer_finish(x, gate, yt, tn, tiles_per_group, final_g)


DEPTH = 2
PAGE_SIZE = 128
S5_WIDTH = D_MODEL // 2
S5_GROUP = 16
S5_GROUPS = S5_WIDTH // S5_GROUP
S5_STATE = 64
NSA_HEADS = 8
NSA_HEAD_DIM = 64
NSA_KV_HEADS = 2
NSA_HPG = NSA_HEADS // NSA_KV_HEADS
NSA_WIDTH = NSA_HEADS * NSA_HEAD_DIM
KV_WIDTH = NSA_KV_HEADS * NSA_HEAD_DIM
CMP_STRIDE = 16
CMP_BLOCK = 2 * CMP_STRIDE
SEL_BLOCK = 64
N_SEL = 16
WINDOW = 512
Q_BLOCK = 128
FORCE_SCORE = 1e4
ATT_SCALE = NSA_HEAD_DIM ** -0.5
HG_HEADS = 8
HG_KEY = D_MODEL // HG_HEADS
HG_VAL = D_MODEL // HG_HEADS
HG_CHUNK = 64


def _rmsnorm(x, g):
    return x * lax.rsqrt(jnp.mean(x * x, axis=-1, keepdims=True) + EPS) * g


def _ada(c, w, b):
    m = jax.nn.silu(c) @ w + b
    return [part[:, None, :] for part in jnp.split(m, 6, axis=-1)]


def _linear_combine(left, right):
    a1, b1 = left
    a2, b2 = right
    return a1 * a2, a2 * b1 + b2


def _s5(u, h0, lam_re, lam_im, log_dt, b_re, b_im, c_re, c_im, d, glu_w, glu_b):
    bsz, t, _ = u.shape
    ug = u.reshape(bsz, t, S5_GROUPS, S5_GROUP)
    lam = lax.complex(lam_re, lam_im)
    dt = jnp.exp(log_dt)[:, None]
    lam_bar = jnp.exp(lam * dt)
    b_bar = ((lam_bar - 1.0) / lam)[..., None] * lax.complex(b_re, b_im)
    bu = jnp.einsum('btgi,gpi->btgp', ug.astype(jnp.complex64), b_bar)
    a = jnp.broadcast_to(lam_bar, bu.shape)
    a_cum, h = lax.associative_scan(_linear_combine, (a, bu), axis=1)
    h = h + a_cum * h0[:, None]
    c = lax.complex(c_re, c_im)
    y = jnp.einsum('btgp,gip->btgi', h, c).real + d.reshape(S5_GROUPS, S5_GROUP) * ug
    y = jax.nn.gelu(y.reshape(bsz, t, S5_WIDTH))
    out = y * jax.nn.sigmoid(y @ glu_w + glu_b)
    return out, h[:, -1]


def _masked_attn(q, k, v, mask):
    s = jnp.einsum('bmqghd,bmkgd->bmghqk', q, k) * ATT_SCALE
    p = jax.nn.softmax(jnp.where(mask[None, :, None, None], s, NEG), axis=-1)
    return jnp.einsum('bmghqk,bmkgd->bmqghd', p, v)


def _window_branch(q, k, v, k_pos0):
    bsz, t = q.shape[:2]
    tw = k.shape[1]
    q_pos0 = k_pos0 + tw - t
    if tw == t and t % Q_BLOCK == 0 and WINDOW % Q_BLOCK == 0:
        nb = t // Q_BLOCK
        r = WINDOW // Q_BLOCK
        pad = ((0, 0), (r * Q_BLOCK, 0), (0, 0), (0, 0))
        kb = jnp.pad(k, pad).reshape(bsz, nb + r, Q_BLOCK, NSA_KV_HEADS, NSA_HEAD_DIM)
        vb = jnp.pad(v, pad).reshape(bsz, nb + r, Q_BLOCK, NSA_KV_HEADS, NSA_HEAD_DIM)
        kband = jnp.concatenate([kb[:, i:i + nb] for i in range(r + 1)], axis=2)
        vband = jnp.concatenate([vb[:, i:i + nb] for i in range(r + 1)], axis=2)
        qpos = q_pos0 + jnp.arange(t).reshape(nb, Q_BLOCK)
        kpos = q_pos0 + (jnp.arange(nb)[:, None] - r) * Q_BLOCK + jnp.arange((r + 1) * Q_BLOCK)[None, :]
        kp = kpos[:, None, :]
        qp = qpos[:, :, None]
        mask = (kp <= qp) & (kp > qp - WINDOW) & (kp >= q_pos0)
        qb = q.reshape(bsz, nb, Q_BLOCK, NSA_KV_HEADS, NSA_HPG, NSA_HEAD_DIM)
        return _masked_attn(qb, kband, vband, mask).reshape(bsz, t, NSA_KV_HEADS, NSA_HPG, NSA_HEAD_DIM)
    qpos = q_pos0 + jnp.arange(t)
    kpos = k_pos0 + jnp.arange(tw)
    mask = (kpos[None, :] <= qpos[:, None]) & (kpos[None, :] > qpos[:, None] - WINDOW)
    return _masked_attn(q[:, None], k[:, None], v[:, None], mask[None])[:, 0]


def _cmp_sel_branch(q, kc_raw, vc_raw, ks_raw, vs_raw, w_ck, w_cv):
    bsz, t = q.shape[:2]
    tk = kc_raw.shape[1]
    q_pos = tk - t + jnp.arange(t)
    nch = -(-tk // CMP_STRIDE)

    def blocks(x):
        x = jnp.pad(x, ((0, 0), (0, nch * CMP_STRIDE - tk), (0, 0), (0, 0)))
        x = x.reshape(bsz, nch, CMP_STRIDE, NSA_KV_HEADS, NSA_HEAD_DIM)
        return jnp.concatenate([x[:, :-1], x[:, 1:]], axis=2)

    kc = jnp.einsum('bnjgd,jde->bnge', blocks(kc_raw), w_ck)
    vc = jnp.einsum('bnjgd,jde->bnge', blocks(vc_raw), w_cv)
    n_cmp = nch - 1
    cmp_start = jnp.arange(n_cmp) * CMP_STRIDE
    mask_c = (cmp_start + CMP_BLOCK - 1)[None, :] <= q_pos[:, None]
    s = jnp.einsum('bqghd,bngd->bghqn', q, kc) * ATT_SCALE
    p = jax.nn.softmax(jnp.where(mask_c, s, NEG), axis=-1) * mask_c.astype(f32)
    o_cmp = jnp.einsum('bghqn,bngd->bqghd', p, vc)
    n_sel = -(-tk // SEL_BLOCK)
    blk = jnp.arange(n_sel)
    sel_start = blk * SEL_BLOCK
    overlap = jnp.clip(jnp.minimum(cmp_start[:, None] + CMP_BLOCK, sel_start[None, :] + SEL_BLOCK)
                       - jnp.maximum(cmp_start[:, None], sel_start[None, :]), 0, None).astype(f32) / CMP_BLOCK
    imp = jnp.einsum('bghqn,nj->bgqj', p, overlap)
    cur = (q_pos // SEL_BLOCK)[:, None]
    valid = sel_start[None, :] <= q_pos[:, None]
    forced = (blk[None, :] == 0) | (blk[None, :] == cur) | (blk[None, :] == cur - 1)
    score = jnp.where(forced, FORCE_SCORE, jnp.where(valid, imp, -1.0))
    n_top = min(N_SEL, n_sel)
    top_s, top_i = lax.top_k(score, n_top)
    top_ok = top_s >= 0.0

    def sel_blocks(x):
        x = jnp.pad(x, ((0, 0), (0, n_sel * SEL_BLOCK - tk), (0, 0), (0, 0)))
        return x.reshape(bsz, n_sel, SEL_BLOCK, NSA_KV_HEADS, NSA_HEAD_DIM).transpose(0, 3, 1, 2, 4)

    ksb = sel_blocks(ks_raw)
    vsb = sel_blocks(vs_raw)
    qb = Q_BLOCK if t % Q_BLOCK == 0 else t
    nq = t // qb
    xs = (q.reshape(bsz, nq, qb, NSA_KV_HEADS, NSA_HPG, NSA_HEAD_DIM).swapaxes(0, 1),
          top_i.reshape(bsz, NSA_KV_HEADS, nq, qb, n_top).transpose(2, 0, 1, 3, 4),
          top_ok.reshape(bsz, NSA_KV_HEADS, nq, qb, n_top).transpose(2, 0, 1, 3, 4),
          q_pos.reshape(nq, qb))
    gather = jax.vmap(jax.vmap(lambda blocks_bg, idx_bg: blocks_bg[idx_bg]))

    def sel_block(args):
        qq, ii, ok, pos = args
        kg = gather(ksb, ii)
        vg = gather(vsb, ii)
        kpos = ii[..., None] * SEL_BLOCK + jnp.arange(SEL_BLOCK)
        m = ok[..., None] & (kpos <= pos[None, None, :, None, None])
        sc = jnp.einsum('bqghd,bgqnjd->bghqnj', qq, kg) * ATT_SCALE
        sc = jnp.where(m[:, :, None], sc, NEG).reshape(bsz, NSA_KV_HEADS, NSA_HPG, qb, n_top * SEL_BLOCK)
        pp = jax.nn.softmax(sc, axis=-1).reshape(bsz, NSA_KV_HEADS, NSA_HPG, qb, n_top, SEL_BLOCK)
        return jnp.einsum('bghqnj,bgqnjd->bqghd', pp, vg)

    o_sel = lax.map(sel_block, xs).swapaxes(0, 1).reshape(bsz, t, NSA_KV_HEADS, NSA_HPG, NSA_HEAD_DIM)
    return o_cmp, o_sel


def _mixer_a(h, past_cmp, past_sel, past_win, s5_h0, w_in, w_out, w_ck, w_cv,
             lam_re, lam_im, log_dt, b_re, b_im, c_re, c_im, d, glu_w, glu_b):
    bsz, t, _ = h.shape
    z = h @ w_in
    sizes = [S5_WIDTH, NSA_WIDTH] + [KV_WIDTH] * 6 + [3 * NSA_HEADS]
    u, qf, kc, vc, ks, vs, kw, vw, gz = jnp.split(z, np.cumsum(sizes)[:-1].tolist(), axis=-1)
    kvr = lambda a: a.reshape(bsz, t, NSA_KV_HEADS, NSA_HEAD_DIM)
    q = qf.reshape(bsz, t, NSA_KV_HEADS, NSA_HPG, NSA_HEAD_DIM)
    gates = jax.nn.sigmoid(gz).reshape(bsz, t, NSA_KV_HEADS, NSA_HPG, 3)
    new_cmp = jnp.stack([kvr(kc), kvr(vc)], axis=2)
    new_sel = jnp.stack([kvr(ks), kvr(vs)], axis=2)
    new_w = jnp.stack([kvr(kw), kvr(vw)], axis=2)
    all_cmp = jnp.concatenate([past_cmp, new_cmp], axis=1)
    all_sel = jnp.concatenate([past_sel, new_sel], axis=1)
    all_win = jnp.concatenate([past_win, new_w], axis=1)
    tk = all_cmp.shape[1]
    o_cmp, o_sel = _cmp_sel_branch(q, all_cmp[:, :, 0], all_cmp[:, :, 1], all_sel[:, :, 0], all_sel[:, :, 1], w_ck, w_cv)
    o_win = _window_branch(q, all_win[:, :, 0], all_win[:, :, 1], tk - all_win.shape[1])
    o = gates[..., 0:1] * o_cmp + gates[..., 1:2] * o_sel + gates[..., 2:3] * o_win
    y_s5, h_last = _s5(u, s5_h0, lam_re, lam_im, log_dt, b_re, b_im, c_re, c_im, d, glu_w, glu_b)
    mixed = jnp.concatenate([y_s5, o.reshape(bsz, t, NSA_WIDTH)], axis=-1)
    out = mixed @ w_out
    new_win = all_win[:, -min(WINDOW, all_win.shape[1]):]
    s5_state = jnp.stack([h_last.real, h_last.imag], axis=-1)
    return out, new_cmp, new_sel, new_win, s5_state


def _hgrn_chunked(q, k, v, logf, s0):
    bsz, t, nh, dk = q.shape
    dv = v.shape[-1]
    L = min(HG_CHUNK, t)
    nc = -(-t // L)
    pad = nc * L - t

    def prep(x):
        x = jnp.pad(x, ((0, 0), (0, pad), (0, 0), (0, 0)))
        return x.reshape(bsz, nc, L, nh, x.shape[-1]).transpose(1, 0, 3, 2, 4)

    causal = jnp.tril(jnp.ones((L, L), dtype=bool))

    def step(S, xs):
        qc, kc, vc, lc = xs
        b = jnp.cumsum(lc, axis=2)
        o_inter = jnp.einsum('bhld,bhde->bhle', qc * jnp.exp(b), S)
        diff = b[:, :, :, None, :] - b[:, :, None, :, :]
        decay = jnp.exp(jnp.where(causal[:, :, None], diff, -jnp.inf))
        att = jnp.einsum('bhtd,bhtsd,bhsd->bhts', qc, decay, kc)
        o = o_inter + jnp.einsum('bhts,bhse->bhte', att, vc)
        b_last = b[:, :, -1:, :]
        S_new = jnp.exp(b[:, :, -1, :])[..., None] * S + jnp.einsum('bhsd,bhse->bhde', kc * jnp.exp(b_last - b), vc)
        return S_new, o

    s_last, o = lax.scan(step, s0, (prep(q), prep(k), prep(v), prep(logf)))
    o = o.transpose(1, 0, 3, 2, 4).reshape(bsz, nc * L, nh, dv)[:, :t]
    return o, s_last


def _mixer_c(h, s0, w_in, w_out, lb, norm_g):
    bsz, t, _ = h.shape
    qf, fz, i_in, g = jnp.split(h @ w_in, 4, axis=-1)
    q = qf.reshape(bsz, t, HG_HEADS, HG_KEY)
    fz = fz.reshape(bsz, t, HG_HEADS, HG_KEY)
    v = i_in.reshape(bsz, t, HG_HEADS, HG_VAL)
    lbh = lb.reshape(HG_HEADS, HG_KEY)
    f = lbh + (1.0 - lbh) * jax.nn.sigmoid(fz)
    k = (1.0 - lbh) * jax.nn.sigmoid(-fz)
    o, s_last = _hgrn_chunked(q, k, v, jnp.log(f), s0)
    o = o * lax.rsqrt(jnp.mean(o * o, axis=-1, keepdims=True) + EPS) * norm_g
    o = o * jax.nn.silu(g.reshape(bsz, t, HG_HEADS, HG_VAL))
    return o.reshape(bsz, t, HG_HEADS * HG_VAL) @ w_out, s_last


PROMPT_TOKEN_TILE = 512


def kernel(x_prompt, x_sample, cache_cmp_kv, cache_sel_kv, cache_win_kv, state_s5, state_hgrn, page_table, c_prompt, c_sample, ada_w, ada_b, norm_mix_g, norm_ffn_g, norm_final_g, a_w_in, a_w_out, s5_lambda_re, s5_lambda_im, s5_log_dt, s5_b_re, s5_b_im, s5_c_re, s5_c_im, s5_d, s5_glu_w, s5_glu_b, nsa_cmp_wk, nsa_cmp_wv, c_w_in, c_w_out, hgrn_lb, hgrn_norm_g, peer_wq, peer_k1, peer_k2, peer_u, peer_v):
    dec_b, n_pages = page_table.shape
    bsz, seq, d = x_prompt.shape
    dec_t = x_sample.shape[1]
    n_p, n_s = bsz * seq, dec_b * dec_t
    tn_p = PROMPT_TOKEN_TILE
    tiles_p = seq // tn_p

    def gather_past(pool):
        rows = pool[page_table]
        return rows.reshape(dec_b, n_pages * PAGE_SIZE, 2, NSA_KV_HEADS, NSA_HEAD_DIM)

    lb_sm = jax.nn.softmax(hgrn_lb, axis=0)
    lb_all = jnp.cumsum(lb_sm, axis=0) - lb_sm[0]

    def per_token(m):
        return jnp.broadcast_to(m, (dec_b, dec_t, d)).reshape(1, n_s, d)

    xp, xs = x_prompt, x_sample
    outs = {k: [] for k in ('cmp_p', 'cmp_s', 'sel_p', 'sel_s', 'win_p', 'win_s', 's5_p', 's5_s', 'hg_p', 'hg_s')}
    for l in range(DEPTH):
        shp_m, scp_m, gtp_m, shp_f, scp_f, gtp_f = _ada(c_prompt, ada_w[l], ada_b[l])
        shs_m, scs_m, gts_m, shs_f, scs_f, gts_f = _ada(c_sample, ada_w[l], ada_b[l])
        hp = _rmsnorm(xp, norm_mix_g[l]) * (1.0 + scp_m) + shp_m
        hs = _rmsnorm(xs, norm_mix_g[l]) * (1.0 + scs_m) + shs_m
        if l % 2 == 0:
            a = l // 2
            wa = (a_w_in[a], a_w_out[a], nsa_cmp_wk[a], nsa_cmp_wv[a], s5_lambda_re[a], s5_lambda_im[a], s5_log_dt[a],
                  s5_b_re[a], s5_b_im[a], s5_c_re[a], s5_c_im[a], s5_d[a], s5_glu_w[a], s5_glu_b[a])
            empty = jnp.zeros((bsz, 0, 2, NSA_KV_HEADS, NSA_HEAD_DIM), f32)
            h0p = jnp.zeros((bsz, S5_GROUPS, S5_STATE), jnp.complex64)
            op, n_cp, n_sp, n_wp, n_5p = _mixer_a(hp, empty, empty, empty, h0p, *wa)
            st = state_s5[a]
            h0s = lax.complex(st[..., 0], st[..., 1])
            os_, n_cs, n_ss, n_ws, n_5s = _mixer_a(hs, gather_past(cache_cmp_kv[a]), gather_past(cache_sel_kv[a]),
                                                   cache_win_kv[a], h0s, *wa)
            for k, v in zip(('cmp_p', 'cmp_s', 'sel_p', 'sel_s', 'win_p', 'win_s', 's5_p', 's5_s'),
                            (n_cp, n_cs, n_sp, n_ss, n_wp, n_ws, n_5p, n_5s)):
                outs[k].append(v)
        else:
            j = l // 2
            wc = (c_w_in[j], c_w_out[j], lb_all[l], hgrn_norm_g[j])
            op, n_hp = _mixer_c(hp, jnp.zeros((bsz, HG_HEADS, HG_KEY, HG_VAL), f32), *wc)
            os_, n_hs = _mixer_c(hs, state_hgrn[j], *wc)
            outs['hg_p'].append(n_hp)
            outs['hg_s'].append(n_hs)
        xp = xp + gtp_m * op
        xs = xs + gts_m * os_
        final_g = norm_final_g.reshape(1, d) if l == DEPTH - 1 else None
        pw = _peer_weights(peer_wq[l], peer_k1[l], peer_k2[l], peer_u[l], peer_v[l])
        ng = norm_ffn_g[l].reshape(1, d)
        xp = _peer_layer(xp.reshape(n_p, d), scp_f, shp_f, gtp_f, ng, pw, tn_p, tiles_p, final_g).reshape(bsz, seq, d)
        xs = _peer_layer(xs.reshape(n_s, d), per_token(scs_f), per_token(shs_f), per_token(gts_f), ng, pw,
                         n_s, 1, final_g).reshape(dec_b, dec_t, d)
    st = lambda k: jnp.stack(outs[k])
    return (xp, xs, st('cmp_p'), st('cmp_s'), st('sel_p'), st('sel_s'), st('win_p'), st('win_s'),
            st('s5_p'), st('s5_s'), st('hg_p'), st('hg_s'))
```
